```python
import math
import jax, jax.numpy as jnp
from jax import lax
import numpy as np

D_MODEL = 2048
BATCH = 4
SEQ = 2048
DEPTH = 4
DEC_BATCH = 128
DEC_SEQ = 1
PAST_LEN = 16384
PAGE_SIZE = 128

D_MIX = D_MODEL
LRU_WIDTH = D_MIX // 2
LRU_HEADS = 8
LRU_BLOCK = LRU_WIDTH // LRU_HEADS
LRU_C = 8.0
CONV_W = 4
GDN_HEADS = 8
GDN_DK = 128
GDN_DV = (D_MIX - LRU_WIDTH) // GDN_HEADS
GDN_QK_WIDTH = GDN_HEADS * GDN_DK
GDN_V_WIDTH = GDN_HEADS * GDN_DV
GDN_CONV_DIM = 2 * GDN_QK_WIDTH + GDN_V_WIDTH
GDN_CHUNK = 64
IN_COLS = 2 * LRU_WIDTH + GDN_CONV_DIM + GDN_V_WIDTH + 2 * GDN_HEADS
N_MEM = 256
XA_HEADS = 4
XA_HEAD_DIM = D_MODEL // XA_HEADS
XA_WIDTH = XA_HEADS * XA_HEAD_DIM
D_FF = -(-8 * D_MODEL // (3 * 256)) * 256
EPS = 1e-6

kernel_name = 'hymba_rglru_gdn_memxattn_decoder_step'

F32 = jnp.float32


def rmsnorm(x, w):
    x32 = x.astype(F32)
    y = x32 * lax.rsqrt(jnp.mean(x32 * x32, axis=-1, keepdims=True) + EPS)
    return (y * w.astype(F32)).astype(x.dtype)


def l2norm(x):
    return x * lax.rsqrt(jnp.sum(x * x, axis=-1, keepdims=True) + EPS)


def causal_conv(x, buf, w):
    L = x.shape[1]
    xp = jnp.concatenate([buf.astype(x.dtype), x], axis=1)
    y = xp[:, 0:L] * w[0]
    for j in range(1, CONV_W):
        y = y + xp[:, j:j + L] * w[j]
    return y, xp[:, L:]


def _lin_combine(c1, c2):
    a1, b1 = c1
    a2, b2 = c2
    return a1 * a2, a2 * b1 + b2


def rg_lru(x, h0, gx_w, gx_b, ga_w, ga_b, lam):
    B, L, W = x.shape
    x32 = x.astype(F32)
    xb = x32.reshape(B, L, LRU_HEADS, LRU_BLOCK)
    gate_x = jax.nn.sigmoid(jnp.einsum('blhi,hij->blhj', xb, gx_w.astype(F32)).reshape(B, L, W) + gx_b.astype(F32))
    gate_a = jax.nn.sigmoid(jnp.einsum('blhi,hij->blhj', xb, ga_w.astype(F32)).reshape(B, L, W) + ga_b.astype(F32))
    log_a = -LRU_C * gate_a * jax.nn.softplus(-lam.astype(F32))
    a = jnp.exp(log_a)
    b = x32 * gate_x * jnp.sqrt(-jnp.expm1(2.0 * log_a))
    b = b.at[:, 0].add(a[:, 0] * h0.astype(F32))
    _, h = lax.associative_scan(_lin_combine, (a, b), axis=1)
    return h.astype(x.dtype), h[:, -1].astype(x.dtype)


def _to_chunks(t, n, C, pad):
    t = jnp.pad(t, [(0, 0), (0, pad)] + [(0, 0)] * (t.ndim - 2))
    t = t.reshape((t.shape[0], n, C) + t.shape[2:])
    perm = (1, 0, 3, 2) + tuple(range(4, t.ndim))
    return t.transpose(perm)


def gated_delta_rule(q, k, v, g, beta, S0):
    B, L, H, _ = q.shape
    DV = v.shape[-1]
    C = GDN_CHUNK if L >= GDN_CHUNK else L
    n = -(-L // C)
    pad = n * C - L
    q, k, v, g, beta = [_to_chunks(t.astype(F32), n, C, pad) for t in (q, k, v, g, beta)]
    gc = jnp.cumsum(g, axis=-1)
    idx = jnp.arange(C)
    incl = idx[:, None] >= idx[None, :]
    strict = idx[:, None] > idx[None, :]
    decay = jnp.exp(jnp.where(incl, gc[..., :, None] - gc[..., None, :], -jnp.inf))
    kb = k * beta[..., None]
    A = jnp.where(strict, jnp.einsum('nbhid,nbhjd->nbhij', kb, k) * decay, 0.0)
    T = A + jnp.eye(C, dtype=F32)
    rhs = jnp.concatenate([v * beta[..., None], kb * jnp.exp(gc)[..., None]], axis=-1)
    sol = lax.linalg.triangular_solve(T, rhs, left_side=True, lower=True, unit_diagonal=True)
    u, w = sol[..., :DV], sol[..., DV:]
    qk = jnp.where(incl, jnp.einsum('nbhid,nbhjd->nbhij', q, k) * decay, 0.0)

    def step(S, inp):
        q_c, k_c, u_c, w_c, g_c, qk_c = inp
        v_new = u_c - jnp.einsum('bhck,bhkv->bhcv', w_c, S)
        o = jnp.einsum('bhck,bhkv->bhcv', q_c * jnp.exp(g_c)[..., None], S) + jnp.einsum('bhij,bhjv->bhiv', qk_c, v_new)
        g_last = g_c[..., -1:]
        S = S * jnp.exp(g_last)[..., None] + jnp.einsum('bhck,bhcv->bhkv', k_c * jnp.exp(g_last - g_c)[..., None], v_new)
        return S, o

    S, o = lax.scan(step, S0.astype(F32), (q, k, u, w, gc, qk))
    o = o.transpose(1, 0, 3, 2, 4).reshape(B, n * C, H, DV)[:, :L]
    return o, S


def mixer(xn, lru_conv, lru_h, gdn_conv, gdn_S, p):
    B, L, _ = xn.shape
    proj = xn @ p['w_in']
    o1 = LRU_WIDTH
    o2 = 2 * LRU_WIDTH
    o3 = o2 + GDN_CONV_DIM
    o4 = o3 + GDN_V_WIDTH
    o5 = o4 + GDN_HEADS
    lru_x, lru_y, qkv = proj[..., :o1], proj[..., o1:o2], proj[..., o2:o3]
    z, b_raw, a_raw = proj[..., o3:o4], proj[..., o4:o5], proj[..., o5:]
    lx, lru_conv = causal_conv(lru_x, lru_conv, p['lru_conv_w'])
    lx = lx + p['lru_conv_b']
    lh, lru_h = rg_lru(lx, lru_h, p['lru_gate_x_w'], p['lru_gate_x_b'], p['lru_gate_a_w'], p['lru_gate_a_b'], p['lru_lambda'])
    lru_out = lh * jax.nn.gelu(lru_y)
    qkv, gdn_conv = causal_conv(qkv, gdn_conv, p['gdn_conv_w'])
    qkv = jax.nn.silu(qkv).astype(F32)
    q = l2norm(qkv[..., :GDN_QK_WIDTH].reshape(B, L, GDN_HEADS, GDN_DK)) * (GDN_DK ** -0.5)
    k = l2norm(qkv[..., GDN_QK_WIDTH:2 * GDN_QK_WIDTH].reshape(B, L, GDN_HEADS, GDN_DK))
    v = qkv[..., 2 * GDN_QK_WIDTH:].reshape(B, L, GDN_HEADS, GDN_DV)
    beta = jax.nn.sigmoid(b_raw.astype(F32))
    g = -jnp.exp(p['gdn_A_log'].astype(F32)) * jax.nn.softplus(a_raw.astype(F32) + p['gdn_dt_bias'].astype(F32))
    o, S = gated_delta_rule(q, k, v, g, beta, gdn_S)
    o = o * lax.rsqrt(jnp.mean(o * o, axis=-1, keepdims=True) + EPS) * p['gdn_norm_w'].astype(F32)
    o = o * jax.nn.silu(z.astype(F32)).reshape(B, L, GDN_HEADS, GDN_DV)
    mixed = jnp.concatenate([lru_out, o.reshape(B, L, GDN_V_WIDTH).astype(xn.dtype)], axis=-1)
    return mixed @ p['w_out'], lru_conv, lru_h, gdn_conv, S.astype(xn.dtype)


def memory_kv(mem, p):
    B, M, _ = mem.shape
    mn = rmsnorm(mem, p['norm_mem'])
    k = (mn @ p['xa_wk']).reshape(B, M, XA_HEADS, XA_HEAD_DIM)
    v = (mn @ p['xa_wv']).reshape(B, M, XA_HEADS, XA_HEAD_DIM)
    return k, v


def cross_attn(xn, mk, mv, p):
    B, L, _ = xn.shape
    q = (xn @ p['xa_wq']).reshape(B, L, XA_HEADS, XA_HEAD_DIM)
    s = jnp.einsum('blhd,bmhd->bhlm', q.astype(F32), mk.astype(F32)) * (XA_HEAD_DIM ** -0.5)
    pr = jax.nn.softmax(s, axis=-1)
    o = jnp.einsum('bhlm,bmhd->blhd', pr.astype(mv.dtype), mv).reshape(B, L, XA_WIDTH)
    return o @ p['xa_wo']


def swiglu(xn, p):
    return (jax.nn.silu(xn @ p['ffn_w_gate']) * (xn @ p['ffn_w_up'])) @ p['ffn_w_down']


def decoder_layer(x, lru_conv, lru_h, gdn_conv, gdn_S, mk, mv, p):
    m, lru_conv, lru_h, gdn_conv, gdn_S = mixer(rmsnorm(x, p['norm_mix_pre']), lru_conv, lru_h, gdn_conv, gdn_S, p)
    x = x + rmsnorm(m, p['norm_mix_post'])
    c = cross_attn(rmsnorm(x, p['norm_xa_pre']), mk, mv, p)
    x = x + rmsnorm(c, p['norm_xa_post'])
    f = swiglu(rmsnorm(x, p['norm_ffn_pre']), p)
    x = x + rmsnorm(f, p['norm_ffn_post'])
    return x, lru_conv, lru_h, gdn_conv, gdn_S


def setup_inputs(seed: int = 0) -> dict:
    key = jax.random.key(seed)
    ks = iter(jax.random.split(key, 48))

    def nrm(shape, scale):
        return scale * jax.random.normal(next(ks), shape, F32)

    def gain(shape):
        return 1.0 + 0.05 * jax.random.normal(next(ks), shape, F32)

    Ld = DEPTH
    a_c = jax.random.uniform(next(ks), (Ld, LRU_WIDTH), F32, minval=0.9, maxval=0.999)
    s = a_c ** (1.0 / LRU_C)
    lru_lambda = jnp.log(s) - jnp.log1p(-s)
    gdn_A_log = jnp.log(jax.random.uniform(next(ks), (Ld, GDN_HEADS), F32, minval=1.0, maxval=16.0))
    dt = jnp.exp(jax.random.uniform(next(ks), (Ld, GDN_HEADS), F32, minval=math.log(1e-3), maxval=math.log(1e-1)))
    gdn_dt_bias = dt + jnp.log(-jnp.expm1(-dt))
    return {
        'x_prompt': nrm((BATCH, SEQ, D_MODEL), 1.0),
        'x_sample': nrm((DEC_BATCH, DEC_SEQ, D_MODEL), 1.0),
        'mem_prompt': nrm((BATCH, N_MEM, D_MODEL), 1.0),
        'state_lru_conv': nrm((Ld, DEC_BATCH, CONV_W - 1, LRU_WIDTH), 1.0),
        'state_lru_h': nrm((Ld, DEC_BATCH, LRU_WIDTH), 1.0),
        'state_gdn_conv': nrm((Ld, DEC_BATCH, CONV_W - 1, GDN_CONV_DIM), 1.0),
        'state_gdn_S': nrm((Ld, DEC_BATCH, GDN_HEADS, GDN_DK, GDN_DV), 0.1),
        'cache_mem_k': nrm((Ld, DEC_BATCH, N_MEM, XA_HEADS, XA_HEAD_DIM), 1.0),
        'cache_mem_v': nrm((Ld, DEC_BATCH, N_MEM, XA_HEADS, XA_HEAD_DIM), 1.0),
        'norm_mix_pre': gain((Ld, D_MODEL)),
        'norm_mix_post': gain((Ld, D_MODEL)),
        'w_in': nrm((Ld, D_MODEL, IN_COLS), D_MODEL ** -0.5),
        'lru_conv_w': nrm((Ld, CONV_W, LRU_WIDTH), CONV_W ** -0.5),
        'lru_conv_b': nrm((Ld, LRU_WIDTH), 0.01),
        'lru_gate_x_w': nrm((Ld, LRU_HEADS, LRU_BLOCK, LRU_BLOCK), LRU_BLOCK ** -0.5),
        'lru_gate_x_b': nrm((Ld, LRU_WIDTH), 0.01),
        'lru_gate_a_w': nrm((Ld, LRU_HEADS, LRU_BLOCK, LRU_BLOCK), LRU_BLOCK ** -0.5),
        'lru_gate_a_b': nrm((Ld, LRU_WIDTH), 0.01),
        'lru_lambda': lru_lambda,
        'gdn_conv_w': nrm((Ld, CONV_W, GDN_CONV_DIM), CONV_W ** -0.5),
        'gdn_A_log': gdn_A_log,
        'gdn_dt_bias': gdn_dt_bias,
        'gdn_norm_w': gain((Ld, GDN_DV)),
        'w_out': nrm((Ld, D_MIX, D_MODEL), D_MIX ** -0.5),
        'norm_xa_pre': gain((Ld, D_MODEL)),
        'norm_xa_post': gain((Ld, D_MODEL)),
        'norm_mem': gain((Ld, D_MODEL)),
        'xa_wq': nrm((Ld, D_MODEL, XA_WIDTH), D_MODEL ** -0.5),
        'xa_wk': nrm((Ld, D_MODEL, XA_WIDTH), D_MODEL ** -0.5),
        'xa_wv': nrm((Ld, D_MODEL, XA_WIDTH), D_MODEL ** -0.5),
        'xa_wo': nrm((Ld, XA_WIDTH, D_MODEL), XA_WIDTH ** -0.5),
        'norm_ffn_pre': gain((Ld, D_MODEL)),
        'norm_ffn_post': gain((Ld, D_MODEL)),
        'ffn_w_gate': nrm((Ld, D_MODEL, D_FF), D_MODEL ** -0.5),
        'ffn_w_up': nrm((Ld, D_MODEL, D_FF), D_MODEL ** -0.5),
        'ffn_w_down': nrm((Ld, D_FF, D_MODEL), D_FF ** -0.5),
    }


def reference(x_prompt, x_sample, mem_prompt, state_lru_conv, state_lru_h, state_gdn_conv, state_gdn_S,
              cache_mem_k, cache_mem_v, norm_mix_pre, norm_mix_post, w_in, lru_conv_w, lru_conv_b,
              lru_gate_x_w, lru_gate_x_b, lru_gate_a_w, lru_gate_a_b, lru_lambda, gdn_conv_w, gdn_A_log,
              gdn_dt_bias, gdn_norm_w, w_out, norm_xa_pre, norm_xa_post, norm_mem, xa_wq, xa_wk, xa_wv,
              xa_wo, norm_ffn_pre, norm_ffn_post, ffn_w_gate, ffn_w_up, ffn_w_down):
    weights = {
        'norm_mix_pre': norm_mix_pre, 'norm_mix_post': norm_mix_post, 'w_in': w_in,
        'lru_conv_w': lru_conv_w, 'lru_conv_b': lru_conv_b,
        'lru_gate_x_w': lru_gate_x_w, 'lru_gate_x_b': lru_gate_x_b,
        'lru_gate_a_w': lru_gate_a_w, 'lru_gate_a_b': lru_gate_a_b, 'lru_lambda': lru_lambda,
        'gdn_conv_w': gdn_conv_w, 'gdn_A_log': gdn_A_log, 'gdn_dt_bias': gdn_dt_bias,
        'gdn_norm_w': gdn_norm_w, 'w_out': w_out,
        'norm_xa_pre': norm_xa_pre, 'norm_xa_post': norm_xa_post, 'norm_mem': norm_mem,
        'xa_wq': xa_wq, 'xa_wk': xa_wk, 'xa_wv': xa_wv, 'xa_wo': xa_wo,
        'norm_ffn_pre': norm_ffn_pre, 'norm_ffn_post': norm_ffn_post,
        'ffn_w_gate': ffn_w_gate, 'ffn_w_up': ffn_w_up, 'ffn_w_down': ffn_w_down,
    }
    dt = x_prompt.dtype
    bp = x_prompt.shape[0]
    zero_lc = jnp.zeros((bp, CONV_W - 1, LRU_WIDTH), dt)
    zero_lh = jnp.zeros((bp, LRU_WIDTH), dt)
    zero_gc = jnp.zeros((bp, CONV_W - 1, GDN_CONV_DIM), dt)
    zero_gs = jnp.zeros((bp, GDN_HEADS, GDN_DK, GDN_DV), dt)
    xp = x_prompt
    xs = x_sample
    p_lc, p_lh, p_gc, p_gs, p_mk, p_mv = [], [], [], [], [], []
    s_lc, s_lh, s_gc, s_gs = [], [], [], []
    for l in range(DEPTH):
        p = {name: w[l] for name, w in weights.items()}
        mk, mv = memory_kv(mem_prompt, p)
        xp, lc, lh, gc, gs = decoder_layer(xp, zero_lc, zero_lh, zero_gc, zero_gs, mk, mv, p)
        p_lc.append(lc); p_lh.append(lh); p_gc.append(gc); p_gs.append(gs); p_mk.append(mk); p_mv.append(mv)
        xs, lc, lh, gc, gs = decoder_layer(xs, state_lru_conv[l], state_lru_h[l], state_gdn_conv[l], state_gdn_S[l],
                                           cache_mem_k[l], cache_mem_v[l], p)
        s_lc.append(lc); s_lh.append(lh); s_gc.append(gc); s_gs.append(gs)
    return (xp, xs,
            jnp.stack(p_lc), jnp.stack(p_lh), jnp.stack(p_gc), jnp.stack(p_gs), jnp.stack(p_mk), jnp.stack(p_mv),
            jnp.stack(s_lc), jnp.stack(s_lh), jnp.stack(s_gc), jnp.stack(s_gs))
```

```python
import functools

import jax
import jax.numpy as jnp
from jax import lax
from jax.experimental import pallas as pl
from jax.experimental.pallas import tpu as pltpu

F32 = jnp.float32
BF16 = jnp.bfloat16
HIGHEST = lax.Precision.HIGHEST

D_MODEL = 2048
DEPTH = 4
LRU_WIDTH = 1024
LRU_HEADS = 8
LRU_BLOCK = LRU_WIDTH // LRU_HEADS
LRU_C = 8.0
CONV_W = 4
GDN_HEADS = 8
GDN_DK = 128
GDN_DV = 128
GDN_QK_WIDTH = GDN_HEADS * GDN_DK
GDN_V_WIDTH = GDN_HEADS * GDN_DV
GDN_CONV_DIM = 2 * GDN_QK_WIDTH + GDN_V_WIDTH
GDN_CHUNK = 64
PROJ_MAIN = 2 * LRU_WIDTH + GDN_CONV_DIM + GDN_V_WIDTH
XA_HEADS = 4
XA_HEAD_DIM = D_MODEL // XA_HEADS
EPS = 1e-6

LANES = 128
SUBLANES = 8
HIST = CONV_W - 1
HIST_ROW0 = SUBLANES - HIST
VMEM_LIMIT = 56 * 1024 * 1024


def _params(*sem):
    return pltpu.CompilerParams(dimension_semantics=sem, vmem_limit_bytes=VMEM_LIMIT)


def _dot(a, b, precision=None):
    return jnp.dot(a, b, preferred_element_type=F32, precision=precision)


def _dot_nt(a, b):
    return lax.dot_general(a, b, (((1,), (1,)), ((), ())), preferred_element_type=F32)


def _dot_tn(a, b):
    return lax.dot_general(a, b, (((0,), (0,)), ((), ())), preferred_element_type=F32)


def _rms(x, g):
    return x * lax.rsqrt(jnp.mean(x * x, axis=-1, keepdims=True) + EPS) * g


def _silu(x):
    return x * jax.nn.sigmoid(x)


def _expm1(x):
    u = jnp.exp(x)
    um1 = u - 1.0
    tiny = um1 == 0.0
    near = jnp.where(tiny, x, um1 * x / jnp.where(tiny, 1.0, jnp.log(u)))
    return jnp.where(x < -0.5, um1, near)


def _norm_cast_kernel(x_ref, g_ref, o_ref):
    o_ref[...] = _rms(x_ref[...], g_ref[...]).astype(o_ref.dtype)


def _norm_cast(x, g, l, tm):
    m, d = x.shape
    return pl.pallas_call(
        _norm_cast_kernel,
        out_shape=jax.ShapeDtypeStruct((m, d), BF16),
        grid=(m // tm,),
        in_specs=[pl.BlockSpec((tm, d), lambda i: (i, 0)),
                  pl.BlockSpec((None, 1, d), lambda i: (l, 0, 0))],
        out_specs=pl.BlockSpec((tm, d), lambda i: (i, 0)),
        compiler_params=_params("parallel"),
        name="norm_cast",
    )(x, g)


def _mm_kernel(a_ref, w_ref, o_ref):
    o_ref[...] = _dot(a_ref[...], w_ref[...]).astype(o_ref.dtype)


def _mm(a, w, l, n_out, tm, tn, out_dtype):
    m, k = a.shape
    return pl.pallas_call(
        _mm_kernel,
        out_shape=jax.ShapeDtypeStruct((m, n_out), out_dtype),
        grid=(m // tm, n_out // tn),
        in_specs=[pl.BlockSpec((tm, k), lambda i, j: (i, 0)),
                  pl.BlockSpec((None, k, tn), lambda i, j: (l, 0, j))],
        out_specs=pl.BlockSpec((tm, tn), lambda i, j: (i, j)),
        compiler_params=_params("parallel", "parallel"),
        name="mm",
    )(a, w)


def _swiglu_kernel(a_ref, wg_ref, wu_ref, o_ref):
    a = a_ref[...]
    g = _dot(a, wg_ref[...])
    u = _dot(a, wu_ref[...])
    o_ref[...] = (_silu(g) * u).astype(o_ref.dtype)


def _swiglu(a, wg, wu, l, tm, tn):
    m, k = a.shape
    n = wg.shape[-1]
    return pl.pallas_call(
        _swiglu_kernel,
        out_shape=jax.ShapeDtypeStruct((m, n), BF16),
        grid=(m // tm, n // tn),
        in_specs=[pl.BlockSpec((tm, k), lambda i, j: (i, 0)),
                  pl.BlockSpec((None, k, tn), lambda i, j: (l, 0, j)),
                  pl.BlockSpec((None, k, tn), lambda i, j: (l, 0, j))],
        out_specs=pl.BlockSpec((tm, tn), lambda i, j: (i, j)),
        compiler_params=_params("parallel", "parallel"),
        name="swiglu",
    )(a, wg, wu)


def _mm_post_kernel(a_ref, w_ref, x_ref, gp_ref, gn_ref, xo_ref, xn_ref, *acc, nk):
    def finish(y):
        xnew = x_ref[...] + _rms(y, gp_ref[...])
        xo_ref[...] = xnew
        xn_ref[...] = _rms(xnew, gn_ref[...]).astype(xn_ref.dtype)

    if nk == 1:
        finish(_dot(a_ref[...], w_ref[...]))
        return
    acc_ref, = acc
    k = pl.program_id(1)

    @pl.when(k == 0)
    def _():
        acc_ref[...] = jnp.zeros_like(acc_ref)

    acc_ref[...] += _dot(a_ref[...], w_ref[...])

    @pl.when(k == nk - 1)
    def _():
        finish(acc_ref[...])


def _mm_post(a, w, l, x, g_post, g_next, l_next, tm, tk):
    m, kdim = a.shape
    d = x.shape[1]
    nk = kdim // tk
    return pl.pallas_call(
        functools.partial(_mm_post_kernel, nk=nk),
        out_shape=(jax.ShapeDtypeStruct((m, d), F32), jax.ShapeDtypeStruct((m, d), BF16)),
        grid=(m // tm, nk),
        in_specs=[pl.BlockSpec((tm, tk), lambda i, k: (i, k)),
                  pl.BlockSpec((None, tk, d), lambda i, k: (l, k, 0)),
                  pl.BlockSpec((tm, d), lambda i, k: (i, 0)),
                  pl.BlockSpec((None, 1, d), lambda i, k: (l, 0, 0)),
                  pl.BlockSpec((None, 1, d), lambda i, k: (l_next, 0, 0))],
        out_specs=(pl.BlockSpec((tm, d), lambda i, k: (i, 0)),
                   pl.BlockSpec((tm, d), lambda i, k: (i, 0))),
        scratch_shapes=[] if nk == 1 else [pltpu.VMEM((tm, d), F32)],
        compiler_params=_params("parallel", "arbitrary"),
        name="mm_post",
    )(a, w, x, g_post, g_next)


def _lru_gates(lxh, gxw, gaw, gxb, gab, sp):
    xb = lxh.astype(BF16)
    gate_x = jax.nn.sigmoid(_dot(xb, gxw) + gxb)
    gate_a = jax.nn.sigmoid(_dot(xb, gaw) + gab)
    log_a = (-LRU_C) * gate_a * sp
    a = jnp.exp(log_a)
    b = lxh * gate_x * jnp.sqrt(-_expm1(2.0 * log_a))
    return a, b


def _lru_seq_kernel(x_ref, y_ref, cs_ref, h0_ref, cw_ref, cb_ref, gxw_ref, gxb_ref, gaw_ref, gab_ref, lam_ref,
                    out_ref, cso_ref, ho_ref, xp_ref, h_ref, *, tt, nc):
    c = pl.program_id(1)

    @pl.when(c == 0)
    def _():
        xp_ref[pl.ds(HIST_ROW0, HIST), :] = cs_ref[...]
        h_ref[...] = h0_ref[...]

    x = x_ref[...]
    xp_ref[pl.ds(SUBLANES, tt), :] = x
    lx = cb_ref[...] + cw_ref[HIST:CONV_W, :] * x
    for j in range(HIST):
        lx = lx + cw_ref[j:j + 1, :] * xp_ref[pl.ds(HIST_ROW0 + j, tt), :]
    xp_ref[pl.ds(HIST_ROW0, HIST), :] = x_ref[pl.ds(tt - HIST, HIST), :]

    row = lax.broadcasted_iota(jnp.int32, (tt, LRU_BLOCK), 0)
    sp = jax.nn.softplus(-lam_ref[...])
    for h in range(LRU_HEADS):
        sl = slice(h * LRU_BLOCK, (h + 1) * LRU_BLOCK)
        a, b = _lru_gates(lx[:, sl], gxw_ref[h], gaw_ref[h], gxb_ref[:, sl], gab_ref[:, sl], sp[:, sl])
        s = 1
        while s < tt:
            keep = row >= s
            b = jnp.where(keep, a * pltpu.roll(b, s, 0) + b, b)
            a = jnp.where(keep, a * pltpu.roll(a, s, 0), a)
            s *= 2
        hs = a * h_ref[:, sl] + b
        h_ref[:, sl] = hs[tt - 1:tt, :]
        out_ref[:, sl] = (hs * jax.nn.gelu(y_ref[:, sl])).astype(out_ref.dtype)

    @pl.when(c == nc - 1)
    def _():
        cso_ref[...] = x_ref[pl.ds(tt - HIST, HIST), :]
        ho_ref[...] = h_ref[...]


def _lru_seq(proj, conv_state, h0, lw, l, n_tok_total, batch, seq, tt):
    nc = seq // tt
    w = LRU_WIDTH
    vec = pl.BlockSpec((None, 1, w), lambda b, c: (l, 0, 0))
    gate_w = pl.BlockSpec((None, LRU_HEADS, LRU_BLOCK, LRU_BLOCK), lambda b, c: (l, 0, 0, 0))
    return pl.pallas_call(
        functools.partial(_lru_seq_kernel, tt=tt, nc=nc),
        out_shape=(jax.ShapeDtypeStruct((n_tok_total, 2 * w), BF16),
                   jax.ShapeDtypeStruct((batch, HIST, w), F32),
                   jax.ShapeDtypeStruct((batch, 1, w), F32)),
        grid=(batch, nc),
        in_specs=[pl.BlockSpec((tt, w), lambda b, c: (b * nc + c, 0)),
                  pl.BlockSpec((tt, w), lambda b, c: (b * nc + c, 1)),
                  pl.BlockSpec((None, HIST, w), lambda b, c: (b, 0, 0)),
                  pl.BlockSpec((None, 1, w), lambda b, c: (b, 0, 0)),
                  pl.BlockSpec((None, CONV_W, w), lambda b, c: (l, 0, 0)),
                  vec, gate_w, vec, gate_w, vec, vec],
        out_specs=(pl.BlockSpec((tt, w), lambda b, c: (b * nc + c, 0)),
                   pl.BlockSpec((None, HIST, w), lambda b, c: (b, 0, 0)),
                   pl.BlockSpec((None, 1, w), lambda b, c: (b, 0, 0))),
        scratch_shapes=[pltpu.VMEM((SUBLANES + tt, w), F32), pltpu.VMEM((1, w), F32)],
        compiler_params=_params("parallel", "arbitrary"),
        name="lru_seq",
    )(proj, proj, conv_state, h0, lw["conv_w"], lw["conv_b"], lw["gate_x_w"], lw["gate_x_b"],
      lw["gate_a_w"], lw["gate_a_b"], lw["lam"])


def _l2norm(x):
    return x * lax.rsqrt(jnp.sum(x * x, axis=-1, keepdims=True) + EPS)


def _gdn_seq_kernel(q_ref, k_ref, v_ref, z_ref, ba_ref, mixed_hbm, cs_ref, s0_ref, cw_ref, alog_ref, dt_ref, nw_ref,
                    out_ref, cso_ref, so_ref, xp_ref, s_ref, *, nc):
    del mixed_hbm
    ch = GDN_CHUNK
    c = pl.program_id(1)

    @pl.when(c == 0)
    def _():
        xp_ref[pl.ds(HIST_ROW0, HIST), :] = cs_ref[...]
        s_ref[...] = s0_ref[...]

    for p, r in enumerate((q_ref, k_ref, v_ref)):
        xp_ref[pl.ds(SUBLANES, ch), p * GDN_QK_WIDTH:(p + 1) * GDN_QK_WIDTH] = r[...]

    def conv_silu(part, h):
        lo = part * GDN_QK_WIDTH + h * GDN_DK
        y = cw_ref[0:1, lo:lo + GDN_DK] * xp_ref[pl.ds(HIST_ROW0, ch), lo:lo + GDN_DK]
        for j in range(1, CONV_W):
            y = y + cw_ref[j:j + 1, lo:lo + GDN_DK] * xp_ref[pl.ds(HIST_ROW0 + j, ch), lo:lo + GDN_DK]
        return _silu(y)

    ba = ba_ref[...]
    beta_all = jax.nn.sigmoid(ba)
    g_all = -jnp.exp(alog_ref[...]) * jax.nn.softplus(ba + dt_ref[...])
    ri = lax.broadcasted_iota(jnp.int32, (ch, ch), 0)
    ci = lax.broadcasted_iota(jnp.int32, (ch, ch), 1)
    incl = ri >= ci
    strict = ri > ci
    eye = (ri == ci).astype(F32)
    gc_all = _dot(incl.astype(F32), g_all, HIGHEST)
    gc_t = gc_all.T

    for h in range(GDN_HEADS):
        sl = slice(h * GDN_DV, (h + 1) * GDN_DV)
        beta = beta_all[:, h:h + 1]
        gc = gc_all[:, GDN_HEADS + h:GDN_HEADS + h + 1]
        gc_row = gc_t[GDN_HEADS + h:GDN_HEADS + h + 1, :]
        decay = jnp.exp(jnp.where(incl, gc - gc_row, -jnp.inf))
        q = _l2norm(conv_silu(0, h)) * (GDN_DK ** -0.5)
        k = _l2norm(conv_silu(1, h))
        v = conv_silu(2, h)
        kb = k * beta
        k16 = k.astype(BF16)
        a_mat = jnp.where(strict, _dot_nt(kb.astype(BF16), k16) * decay, 0.0)
        qk = _dot_nt(q.astype(BF16), k16) * decay
        n_pow = -a_mat
        t_inv = eye + n_pow
        steps = ch.bit_length() - 2
        for _ in range(steps):
            n_pow = _dot(n_pow, n_pow, HIGHEST)
            t_inv = t_inv + _dot(t_inv, n_pow, HIGHEST)
        eg = jnp.exp(gc)
        sol = _dot(t_inv, jnp.concatenate([v * beta, kb * eg], axis=1), HIGHEST)
        u = sol[:, :GDN_DV]
        w = sol[:, GDN_DV:]
        s_old = s_ref[h]
        s16 = s_old.astype(BF16)
        v_new = u - _dot(w.astype(BF16), s16)
        v16 = v_new.astype(BF16)
        o = _dot((q * eg).astype(BF16), s16) + _dot(qk.astype(BF16), v16)
        g_last = gc[ch - 1:ch, :]
        k_dec = k * jnp.exp(g_last - gc)
        s_ref[h] = s_old * jnp.exp(g_last) + _dot_tn(k_dec.astype(BF16), v16)
        on = o * lax.rsqrt(jnp.mean(o * o, axis=-1, keepdims=True) + EPS) * nw_ref[...]
        out_ref[:, sl] = (on * _silu(z_ref[:, sl])).astype(out_ref.dtype)

    xp_ref[pl.ds(HIST_ROW0, HIST), :] = xp_ref[pl.ds(SUBLANES + ch - HIST, HIST), :]

    @pl.when(c == nc - 1)
    def _():
        cso_ref[...] = xp_ref[pl.ds(HIST_ROW0, HIST), :]
        so_ref[...] = s_ref[...]


def _gdn_seq(proj, ba, mixed, conv_state, s0, gw, l, batch, seq):
    nc = seq // GDN_CHUNK
    qw = GDN_QK_WIDTH
    col0 = 2 * LRU_WIDTH // qw
    row = lambda b, c: b * nc + c
    gate_vec = pl.BlockSpec((None, 1, LANES), lambda b, c: (l, 0, 0))
    return pl.pallas_call(
        functools.partial(_gdn_seq_kernel, nc=nc),
        out_shape=(jax.ShapeDtypeStruct(mixed.shape, mixed.dtype),
                   jax.ShapeDtypeStruct((batch, HIST, GDN_CONV_DIM), F32),
                   jax.ShapeDtypeStruct((batch, GDN_HEADS, GDN_DK, GDN_DV), F32)),
        grid=(batch, nc),
        in_specs=[pl.BlockSpec((GDN_CHUNK, qw), lambda b, c: (row(b, c), col0)),
                  pl.BlockSpec((GDN_CHUNK, qw), lambda b, c: (row(b, c), col0 + 1)),
                  pl.BlockSpec((GDN_CHUNK, qw), lambda b, c: (row(b, c), col0 + 2)),
                  pl.BlockSpec((GDN_CHUNK, qw), lambda b, c: (row(b, c), col0 + 3)),
                  pl.BlockSpec((GDN_CHUNK, LANES), lambda b, c: (row(b, c), 0)),
                  pl.BlockSpec(memory_space=pl.ANY),
                  pl.BlockSpec((None, HIST, GDN_CONV_DIM), lambda b, c: (b, 0, 0)),
                  pl.BlockSpec((None, GDN_HEADS, GDN_DK, GDN_DV), lambda b, c: (b, 0, 0, 0)),
                  pl.BlockSpec((None, CONV_W, GDN_CONV_DIM), lambda b, c: (l, 0, 0)),
                  gate_vec, gate_vec,
                  pl.BlockSpec((None, 1, GDN_DV), lambda b, c: (l, 0, 0))],
        out_specs=(pl.BlockSpec((GDN_CHUNK, GDN_V_WIDTH), lambda b, c: (row(b, c), 1)),
                   pl.BlockSpec((None, HIST, GDN_CONV_DIM), lambda b, c: (b, 0, 0)),
                   pl.BlockSpec((None, GDN_HEADS, GDN_DK, GDN_DV), lambda b, c: (b, 0, 0, 0))),
        scratch_shapes=[pltpu.VMEM((SUBLANES + GDN_CHUNK, GDN_CONV_DIM), F32),
                        pltpu.VMEM((GDN_HEADS, GDN_DK, GDN_DV), F32)],
        input_output_aliases={5: 0},
        compiler_params=_params("parallel", "arbitrary"),
        name="gdn_seq",
    )(proj, proj, proj, proj, ba, mixed, conv_state, s0, gw["conv_w"], gw["a_log"], gw["dt_bias"], gw["norm_w"])


def _smp_prep_kernel(lx_ref, ly_ref, q_ref, k_ref, v_ref, ba_ref, lcs_ref, lh_ref, gcs_ref,
                     lcw_ref, lcb_ref, gxw_ref, gxb_ref, gaw_ref, gab_ref, lam_ref, gcw_ref, alog_ref, dt_ref,
                     lout_ref, lcso_ref, lho_ref, gcso_ref, w_ref, qe_ref, ko_ref, u_ref, eg_ref, qk_ref):
    lw = LRU_WIDTH
    x = lx_ref[...]
    lx = lcb_ref[...] + lcw_ref[HIST:CONV_W, :] * x
    for j in range(HIST):
        lx = lx + lcw_ref[j:j + 1, :] * lcs_ref[:, j * lw:(j + 1) * lw]
    lcso_ref[:, 0:(HIST - 1) * lw] = lcs_ref[:, lw:HIST * lw]
    lcso_ref[:, (HIST - 1) * lw:HIST * lw] = x
    sp = jax.nn.softplus(-lam_ref[...])
    for h in range(LRU_HEADS):
        sl = slice(h * LRU_BLOCK, (h + 1) * LRU_BLOCK)
        a, b = _lru_gates(lx[:, sl], gxw_ref[h], gaw_ref[h], gxb_ref[:, sl], gab_ref[:, sl], sp[:, sl])
        hs = a * lh_ref[:, sl] + b
        lho_ref[:, sl] = hs
        lout_ref[:, sl] = (hs * jax.nn.gelu(ly_ref[:, sl])).astype(lout_ref.dtype)

    cd = GDN_CONV_DIM
    ba = ba_ref[...]
    beta_all = jax.nn.sigmoid(ba)
    eg_all = jnp.exp(-jnp.exp(alog_ref[...]) * jax.nn.softplus(ba + dt_ref[...]))
    gcso_ref[:, 0:(HIST - 1) * cd] = gcs_ref[:, cd:HIST * cd]
    for p, r in enumerate((q_ref, k_ref, v_ref)):
        gcso_ref[:, (HIST - 1) * cd + p * GDN_QK_WIDTH:(HIST - 1) * cd + (p + 1) * GDN_QK_WIDTH] = r[...]

    def conv_silu(part, h):
        lo = part * GDN_QK_WIDTH + h * GDN_DK
        r = (q_ref, k_ref, v_ref)[part]
        y = gcw_ref[HIST:CONV_W, lo:lo + GDN_DK] * r[:, h * GDN_DK:(h + 1) * GDN_DK]
        for j in range(HIST):
            y = y + gcw_ref[j:j + 1, lo:lo + GDN_DK] * gcs_ref[:, j * cd + lo:j * cd + lo + GDN_DK]
        return _silu(y)

    for h in range(GDN_HEADS):
        sl = slice(h * GDN_DK, (h + 1) * GDN_DK)
        beta = beta_all[:, h:h + 1]
        eg = eg_all[:, GDN_HEADS + h:GDN_HEADS + h + 1]
        q = _l2norm(conv_silu(0, h)) * (GDN_DK ** -0.5)
        k = _l2norm(conv_silu(1, h))
        v = conv_silu(2, h)
        w_ref[:, sl] = k * (beta * eg)
        qe_ref[:, sl] = q * eg
        ko_ref[:, sl] = k
        u_ref[:, sl] = v * beta
        eg_ref[:, sl] = jnp.broadcast_to(eg, k.shape)
        qk_ref[:, sl] = jnp.broadcast_to(jnp.sum(q * k, axis=-1, keepdims=True), k.shape)


def _smp_prep(proj, ba, lru_conv, lru_h, gdn_conv, lw, gw, l, row_block, n_smp):
    w = LRU_WIDTH
    pcol = lambda j: pl.BlockSpec((n_smp, w), lambda i: (row_block, j))
    lvec = pl.BlockSpec((None, 1, w), lambda i: (l, 0, 0))
    gate_w = pl.BlockSpec((None, LRU_HEADS, LRU_BLOCK, LRU_BLOCK), lambda i: (l, 0, 0, 0))
    gvec = pl.BlockSpec((None, 1, LANES), lambda i: (l, 0, 0))
    full = lambda shape: pl.BlockSpec(shape, lambda i: (0,) * len(shape))
    row_f32 = jax.ShapeDtypeStruct((n_smp, w), F32)
    return pl.pallas_call(
        _smp_prep_kernel,
        out_shape=(jax.ShapeDtypeStruct((n_smp, w), BF16),
                   jax.ShapeDtypeStruct((n_smp, HIST * w), F32),
                   row_f32,
                   jax.ShapeDtypeStruct((n_smp, HIST * GDN_CONV_DIM), F32),
                   row_f32, row_f32, row_f32, row_f32, row_f32, row_f32),
        grid=(1,),
        in_specs=[pcol(0), pcol(1), pcol(2), pcol(3), pcol(4),
                  pl.BlockSpec((n_smp, LANES), lambda i: (row_block, 0)),
                  pl.BlockSpec((None, n_smp, HIST * w), lambda i: (l, 0, 0)),
                  pl.BlockSpec((None, n_smp, w), lambda i: (l, 0, 0)),
                  pl.BlockSpec((None, n_smp, HIST * GDN_CONV_DIM), lambda i: (l, 0, 0)),
                  pl.BlockSpec((None, CONV_W, w), lambda i: (l, 0, 0)),
                  lvec, gate_w, lvec, gate_w, lvec, lvec,
                  pl.BlockSpec((None, CONV_W, GDN_CONV_DIM), lambda i: (l, 0, 0)),
                  gvec, gvec],
        out_specs=(full((n_smp, w)), full((n_smp, HIST * w)), full((n_smp, w)),
                   full((n_smp, HIST * GDN_CONV_DIM)),
                   full((n_smp, w)), full((n_smp, w)), full((n_smp, w)), full((n_smp, w)),
                   full((n_smp, w)), full((n_smp, w))),
        compiler_params=_params("arbitrary"),
        name="smp_prep",
    )(proj, proj, proj, proj, proj, ba, lru_conv, lru_h, gdn_conv,
      lw["conv_w"], lw["conv_b"], lw["gate_x_w"], lw["gate_x_b"], lw["gate_a_w"], lw["gate_a_b"], lw["lam"],
      gw["conv_w"], gw["a_log"], gw["dt_bias"])


def _gdn_state_kernel(w_ref, qe_ref, k_ref, u_ref, eg_ref, qk_ref, s_ref, o_ref, so_ref, *, bb):
    for i in range(bb):
        w_t = w_ref[i].T
        qe_t = qe_ref[i].T
        k_t = k_ref[i].T
        rows = []
        for h in range(GDN_HEADS):
            s_old = s_ref[i, h]
            v_new = u_ref[i, h:h + 1, :] - jnp.sum(w_t[:, h:h + 1] * s_old, axis=0, keepdims=True)
            rows.append(jnp.sum(qe_t[:, h:h + 1] * s_old, axis=0, keepdims=True) + qk_ref[i, h:h + 1, :] * v_new)
            so_ref[i, h] = s_old * eg_ref[i, h:h + 1, :] + k_t[:, h:h + 1] * v_new
        o_ref[i] = jnp.concatenate(rows, axis=0)


def _gdn_state(w3, qe3, k3, u3, eg3, qk3, state, l, bb):
    n_smp = w3.shape[0]
    vec = pl.BlockSpec((bb, GDN_HEADS, GDN_DK), lambda i: (i, 0, 0))
    return pl.pallas_call(
        functools.partial(_gdn_state_kernel, bb=bb),
        out_shape=(jax.ShapeDtypeStruct((n_smp, GDN_HEADS, GDN_DV), F32),
                   jax.ShapeDtypeStruct((n_smp, GDN_HEADS, GDN_DK, GDN_DV), F32)),
        grid=(n_smp // bb,),
        in_specs=[vec, vec, vec, vec, vec, vec,
                  pl.BlockSpec((None, bb, GDN_HEADS, GDN_DK, GDN_DV), lambda i: (l, i, 0, 0, 0))],
        out_specs=(pl.BlockSpec((bb, GDN_HEADS, GDN_DV), lambda i: (i, 0, 0)),
                   pl.BlockSpec((bb, GDN_HEADS, GDN_DK, GDN_DV), lambda i: (i, 0, 0, 0))),
        compiler_params=_params("parallel"),
        name="gdn_state",
    )(w3, qe3, k3, u3, eg3, qk3, state)


def _smp_post_kernel(lout_ref, o_ref, z_ref, nw_ref, mixed_hbm, out_ref):
    del mixed_hbm
    out_ref[:, 0:LRU_WIDTH] = lout_ref[...]
    for h in range(GDN_HEADS):
        sl = slice(h * GDN_DV, (h + 1) * GDN_DV)
        o = o_ref[:, sl]
        on = o * lax.rsqrt(jnp.mean(o * o, axis=-1, keepdims=True) + EPS) * nw_ref[...]
        out_ref[:, LRU_WIDTH + h * GDN_DV:LRU_WIDTH + (h + 1) * GDN_DV] = (on * _silu(z_ref[:, sl])).astype(out_ref.dtype)


def _smp_post(lout, o_rows, proj, norm_w, mixed, l, row_block, n_smp):
    zcol = (2 * LRU_WIDTH + GDN_CONV_DIM) // GDN_V_WIDTH
    return pl.pallas_call(
        _smp_post_kernel,
        out_shape=jax.ShapeDtypeStruct(mixed.shape, mixed.dtype),
        grid=(1,),
        in_specs=[pl.BlockSpec((n_smp, LRU_WIDTH), lambda i: (0, 0)),
                  pl.BlockSpec((n_smp, GDN_V_WIDTH), lambda i: (0, 0)),
                  pl.BlockSpec((n_smp, GDN_V_WIDTH), lambda i: (row_block, zcol)),
                  pl.BlockSpec((None, 1, GDN_DV), lambda i: (l, 0, 0)),
                  pl.BlockSpec(memory_space=pl.ANY)],
        out_specs=pl.BlockSpec((n_smp, mixed.shape[1]), lambda i: (row_block, 0)),
        input_output_aliases={4: 0},
        compiler_params=_params("arbitrary"),
        name="smp_post",
    )(lout, o_rows, proj, norm_w, mixed)


def _xattn_seq_kernel(q_ref, k_ref, v_ref, o_ref):
    s = _dot_nt(q_ref[...], k_ref[...].astype(BF16)) * (XA_HEAD_DIM ** -0.5)
    s = s - jnp.max(s, axis=-1, keepdims=True)
    p = jnp.exp(s)
    p = p / jnp.sum(p, axis=-1, keepdims=True)
    o_ref[...] = _dot(p.astype(BF16), v_ref[...].astype(BF16)).astype(o_ref.dtype)


def _xattn_seq(q, mk, mv, batch, seq, n_mem, tq):
    nq = seq // tq
    hd = XA_HEAD_DIM
    return pl.pallas_call(
        _xattn_seq_kernel,
        out_shape=jax.ShapeDtypeStruct(q.shape, BF16),
        grid=(batch, XA_HEADS, nq),
        in_specs=[pl.BlockSpec((tq, hd), lambda b, h, i: (b * nq + i, h)),
                  pl.BlockSpec((n_mem, hd), lambda b, h, i: (b, h)),
                  pl.BlockSpec((n_mem, hd), lambda b, h, i: (b, h))],
        out_specs=pl.BlockSpec((tq, hd), lambda b, h, i: (b * nq + i, h)),
        compiler_params=_params("parallel", "parallel", "parallel"),
        name="xattn_seq",
    )(q, mk, mv)


def _xattn_step_kernel(q_ref, k_ref, v_ref, attn_hbm, o_ref, q32_ref, o32_ref, *, bb, nb):
    del attn_hbm
    i = pl.program_id(0)
    hd = XA_HEAD_DIM

    @pl.when(i == 0)
    def _():
        q32_ref[...] = q_ref[...].astype(F32)

    for r in range(bb):
        q = q32_ref[pl.ds(i * bb + r, 1), :]
        for h in range(XA_HEADS):
            sl = slice(h * hd, (h + 1) * hd)
            s = jnp.sum(k_ref[r, :, sl] * q[:, sl], axis=-1, keepdims=True) * (hd ** -0.5)
            p = jnp.exp(s - jnp.max(s, axis=0, keepdims=True))
            p = p / jnp.sum(p, axis=0, keepdims=True)
            o32_ref[pl.ds(i * bb + r, 1), sl] = jnp.sum(v_ref[r, :, sl] * p, axis=0, keepdims=True)

    @pl.when(i == nb - 1)
    def _():
        o_ref[...] = o32_ref[...].astype(o_ref.dtype)


def _xattn_step(q, cache_k, cache_v, attn, l, row_block, n_smp, bb):
    nb = n_smp // bb
    n_mem, width = cache_k.shape[2], cache_k.shape[3]
    return pl.pallas_call(
        functools.partial(_xattn_step_kernel, bb=bb, nb=nb),
        out_shape=jax.ShapeDtypeStruct(attn.shape, attn.dtype),
        grid=(nb,),
        in_specs=[pl.BlockSpec((n_smp, width), lambda i: (row_block, 0)),
                  pl.BlockSpec((None, bb, n_mem, width), lambda i: (l, i, 0, 0)),
                  pl.BlockSpec((None, bb, n_mem, width), lambda i: (l, i, 0, 0)),
                  pl.BlockSpec(memory_space=pl.ANY)],
        out_specs=pl.BlockSpec((n_smp, width), lambda i: (row_block, 0)),
        scratch_shapes=[pltpu.VMEM((n_smp, width), F32), pltpu.VMEM((n_smp, width), F32)],
        input_output_aliases={3: 0},
        compiler_params=_params("arbitrary"),
        name="xattn_step",
    )(q, cache_k, cache_v, attn)


def kernel(x_prompt, x_sample, mem_prompt, state_lru_conv, state_lru_h, state_gdn_conv, state_gdn_S, cache_mem_k, cache_mem_v, norm_mix_pre, norm_mix_post, w_in, lru_conv_w, lru_conv_b, lru_gate_x_w, lru_gate_x_b, lru_gate_a_w, lru_gate_a_b, lru_lambda, gdn_conv_w, gdn_A_log, gdn_dt_bias, gdn_norm_w, w_out, norm_xa_pre, norm_xa_post, norm_mem, xa_wq, xa_wk, xa_wv, xa_wo, norm_ffn_pre, norm_ffn_post, ffn_w_gate, ffn_w_up, ffn_w_down):
    batch, seq, d = x_prompt.shape
    n_smp = x_sample.shape[0]
    n_mem = mem_prompt.shape[1]
    n_prompt = batch * seq
    m = n_prompt + n_smp
    smp_block = n_prompt // n_smp
    tm = 640
    tm_post = 320

    bf = lambda w: w.astype(BF16)
    row = lambda v: v.reshape(DEPTH, 1, v.shape[-1])
    w_in16 = bf(w_in)
    w_ba16 = bf(jnp.pad(w_in[:, :, PROJ_MAIN:], ((0, 0), (0, 0), (0, LANES - 2 * GDN_HEADS))))
    w_out16, wq16, wk16, wv16, wo16 = bf(w_out), bf(xa_wq), bf(xa_wk), bf(xa_wv), bf(xa_wo)
    wg16, wu16, wd16 = bf(ffn_w_gate), bf(ffn_w_up), bf(ffn_w_down)
    lw = dict(conv_w=lru_conv_w, conv_b=row(lru_conv_b), gate_x_w=bf(lru_gate_x_w), gate_x_b=row(lru_gate_x_b),
              gate_a_w=bf(lru_gate_a_w), gate_a_b=row(lru_gate_a_b), lam=row(lru_lambda))
    head_pad = ((0, 0), (GDN_HEADS, LANES - 2 * GDN_HEADS))
    gw = dict(conv_w=gdn_conv_w, a_log=row(jnp.pad(gdn_A_log, head_pad)), dt_bias=row(jnp.pad(gdn_dt_bias, head_pad)),
              norm_w=row(gdn_norm_w))
    n_mix_pre, n_mix_post = row(norm_mix_pre), row(norm_mix_post)
    n_xa_pre, n_xa_post, n_mem_w = row(norm_xa_pre), row(norm_xa_post), row(norm_mem)
    n_ffn_pre, n_ffn_post = row(norm_ffn_pre), row(norm_ffn_post)

    x = jnp.concatenate([x_prompt.reshape(n_prompt, d), x_sample.reshape(n_smp, d)], axis=0)
    mem = mem_prompt.reshape(batch * n_mem, d)
    zeros = lambda *s: jnp.zeros(s, F32)
    s_lru_conv = state_lru_conv.reshape(DEPTH, n_smp, HIST * LRU_WIDTH)
    s_gdn_conv = state_gdn_conv.reshape(DEPTH, n_smp, HIST * GDN_CONV_DIM)
    cache_k = cache_mem_k.reshape(DEPTH, n_smp, n_mem, d)
    cache_v = cache_mem_v.reshape(DEPTH, n_smp, n_mem, d)
    heads3 = lambda a: a.reshape(n_smp, GDN_HEADS, GDN_DK)

    outs = [[] for _ in range(10)]
    xn = _norm_cast(x, n_mix_pre, 0, tm)
    for l in range(DEPTH):
        proj = _mm(xn, w_in16, l, PROJ_MAIN, tm, 512, F32)
        ba = _mm(xn, w_ba16, l, LANES, tm, LANES, F32)
        mn = _norm_cast(mem, n_mem_w, l, 256)
        mk = _mm(mn, wk16, l, d, 256, 512, F32)
        mv = _mm(mn, wv16, l, d, 256, 512, F32)

        mixed, p_lc, p_lh = _lru_seq(proj, zeros(batch, HIST, LRU_WIDTH), zeros(batch, 1, LRU_WIDTH), lw, l,
                                     m, batch, seq, 256)
        mixed, p_gc, p_gs = _gdn_seq(proj, ba, mixed, zeros(batch, HIST, GDN_CONV_DIM),
                                     zeros(batch, GDN_HEADS, GDN_DK, GDN_DV), gw, l, batch, seq)
        (lout, s_lc, s_lh, s_gc, w_r, qe_r, k_r, u_r, eg_r, qk_r) = _smp_prep(
            proj, ba, s_lru_conv, state_lru_h, s_gdn_conv, lw, gw, l, smp_block, n_smp)
        o3, s_gs = _gdn_state(heads3(w_r), heads3(qe_r), heads3(k_r), heads3(u_r), heads3(eg_r), heads3(qk_r),
                              state_gdn_S, l, 8)
        mixed = _smp_post(lout, o3.reshape(n_smp, GDN_V_WIDTH), proj, gw["norm_w"], mixed, l, smp_block, n_smp)
        x, xn = _mm_post(mixed, w_out16, l, x, n_mix_post, n_xa_pre, l, tm_post, d)

        q = _mm(xn, wq16, l, d, tm, 512, BF16)
        attn = _xattn_seq(q, mk, mv, batch, seq, n_mem, 512)
        attn = _xattn_step(q, cache_k, cache_v, attn, l, smp_block, n_smp, 2)
        x, xn = _mm_post(attn, wo16, l, x, n_xa_post, n_ffn_pre, l, tm_post, d)

        hidden = _swiglu(xn, wg16, wu16, l, tm, 512)
        x, xn = _mm_post(hidden, wd16, l, x, n_ffn_post, n_mix_pre, (l + 1) % DEPTH, tm_post, 1408)

        for acc, val in zip(outs, (p_lc, p_lh.reshape(batch, LRU_WIDTH), p_gc, p_gs,
                                   mk.reshape(batch, n_mem, XA_HEADS, XA_HEAD_DIM),
                                   mv.reshape(batch, n_mem, XA_HEADS, XA_HEAD_DIM),
                                   s_lc.reshape(n_smp, HIST, LRU_WIDTH), s_lh,
                                   s_gc.reshape(n_smp, HIST, GDN_CONV_DIM), s_gs)):
            acc.append(val)

    y_prompt = x[:n_prompt].reshape(batch, seq, d)
    y_sample = x[n_prompt:].reshape(n_smp, 1, d)
    return (y_prompt, y_sample) + tuple(jnp.stack(o) for o in outs)
```
